```python
import math
import jax, jax.numpy as jnp
from jax import lax
import numpy as np

D_MODEL = 4096
BATCH = 1
SEQ = 8192
DEPTH = 1

POOL_WINDOWS = (2, 4, 8, 16)
POOL_WIDTH = D_MODEL
POOL_GROUP = POOL_WIDTH // len(POOL_WINDOWS)
DN_HEADS = 32
DN_HEAD_DIM = D_MODEL // DN_HEADS
DN_KEY = DN_HEADS * DN_HEAD_DIM
DN_VAL = DN_HEADS * DN_HEAD_DIM
CONV_K = 4
CONV_CH = 2 * DN_KEY + DN_VAL
CHUNK = 64
IN_SPLITS = (POOL_WIDTH, POOL_WIDTH, DN_KEY, DN_KEY, DN_VAL, DN_VAL, DN_HEADS, DN_HEADS, D_MODEL, D_MODEL)
IN_COLS = 4 * D_MODEL + 2 * DN_KEY + 2 * DN_VAL + 2 * DN_HEADS - 2 * D_MODEL + 2 * POOL_WIDTH - 2 * D_MODEL + 0 if False else (2 * POOL_WIDTH + 2 * DN_KEY + 2 * DN_VAL + 2 * DN_HEADS + 2 * D_MODEL)
NORM_EPS = 1e-6
L2_EPS = 1e-6

kernel_name = "hybrid_pool_gated_deltanet_parallel"


def rmsnorm(x, gain):
    xf = x.astype(jnp.float32)
    y = xf * lax.rsqrt(jnp.mean(xf * xf, axis=-1, keepdims=True) + NORM_EPS)
    return (y * gain.astype(jnp.float32)).astype(x.dtype)


def l2norm(x):
    return x * lax.rsqrt(jnp.sum(x * x, axis=-1, keepdims=True) + L2_EPS)


def causal_mean_minus_self(u, window):
    b, s, c = u.shape
    uf = u.astype(jnp.float32)
    csum = jnp.pad(jnp.cumsum(uf, axis=1), ((0, 0), (1, 0), (0, 0)))
    upper = csum[:, 1:]
    lower = jnp.concatenate([jnp.zeros((b, window - 1, c), jnp.float32), csum[:, : s - window + 1]], axis=1)
    count = jnp.minimum(jnp.arange(1, s + 1, dtype=jnp.float32), float(window))
    mean = (upper - lower) / count[None, :, None]
    return (mean - uf).astype(u.dtype)


def causal_depthwise_conv(u, w):
    c = u.shape[-1]
    return lax.conv_general_dilated(
        u, w[:, None, :].astype(u.dtype), window_strides=(1,), padding=[(CONV_K - 1, 0)],
        dimension_numbers=("NWC", "WIO", "NWC"), feature_group_count=c)


def gated_delta_rule_chunked(q, k, v, g, beta):
    b, s, h, dk = q.shape
    dv = v.shape[-1]
    n = s // CHUNK
    to_chunks = lambda t: jnp.moveaxis(t.reshape(b, n, CHUNK, h, -1), 3, 1)
    qc, kc, vc = to_chunks(q), to_chunks(k), to_chunks(v)
    gc = jnp.cumsum(jnp.moveaxis(g.reshape(b, n, CHUNK, h), 3, 1), axis=-1)
    bc = jnp.moveaxis(beta.reshape(b, n, CHUNK, h), 3, 1)

    causal = jnp.tril(jnp.ones((CHUNK, CHUNK), dtype=bool))
    strict = jnp.tril(jnp.ones((CHUNK, CHUNK), dtype=bool), k=-1)
    diff = gc[..., :, None] - gc[..., None, :]
    decay_mat = jnp.where(causal, jnp.exp(jnp.where(causal, diff, 0.0)), 0.0)

    k_beta = kc * bc[..., None]
    a_mat = jnp.where(strict, jnp.einsum("bhnid,bhnjd->bhnij", k_beta, kc) * decay_mat, 0.0)
    m_mat = a_mat + jnp.eye(CHUNK, dtype=jnp.float32)
    rhs = jnp.concatenate([vc * bc[..., None], k_beta * jnp.exp(gc)[..., None]], axis=-1)
    sol = lax.linalg.triangular_solve(m_mat, rhs, left_side=True, lower=True, unit_diagonal=True)
    u_c, w_c = sol[..., :dv], sol[..., dv:]
    qk = jnp.where(causal, jnp.einsum("bhnid,bhnjd->bhnij", qc, kc) * decay_mat, 0.0)

    def step(state, inp):
        q_i, k_i, u_i, w_i, qk_i, g_i = inp
        v_new = u_i - jnp.einsum("bhck,bhkv->bhcv", w_i, state)
        o_i = jnp.einsum("bhck,bhkv->bhcv", q_i * jnp.exp(g_i)[..., None], state) + jnp.einsum("bhij,bhjv->bhiv", qk_i, v_new)
        g_last = g_i[..., -1]
        k_dec = k_i * jnp.exp(g_last[..., None] - g_i)[..., None]
        state = state * jnp.exp(g_last)[..., None, None] + jnp.einsum("bhck,bhcv->bhkv", k_dec, v_new)
        return state, o_i

    xs = tuple(jnp.moveaxis(t, 2, 0) for t in (qc, kc, u_c, w_c, qk, gc))
    state0 = jnp.zeros((b, h, dk, dv), jnp.float32)
    _, o = lax.scan(step, state0, xs)
    return jnp.transpose(o, (1, 0, 3, 2, 4)).reshape(b, s, h, dv)


def setup_inputs(seed: int = 0) -> dict:
    key = jax.random.key(seed)
    ks = jax.random.split(key, 14)
    f = jnp.float32
    nrm = lambda k, shape, scale: jax.random.normal(k, shape, f) * scale
    x = jax.random.normal(ks[0], (BATCH, SEQ, D_MODEL), f)
    norm_gain = 1.0 + nrm(ks[1], (DEPTH, D_MODEL), 0.02)
    w_in = nrm(ks[2], (DEPTH, D_MODEL, IN_COLS), D_MODEL ** -0.5)
    w_qkv_conv = nrm(ks[3], (DEPTH, CONV_K, CONV_CH), CONV_K ** -0.5)
    pool_mix = nrm(ks[4], (DEPTH, len(POOL_WINDOWS), POOL_GROUP, POOL_GROUP), POOL_GROUP ** -0.5)
    pool_scale = 1.0 + nrm(ks[5], (DEPTH, POOL_WIDTH), 0.1)
    w_pool_proj = nrm(ks[6], (DEPTH, POOL_WIDTH, D_MODEL), POOL_WIDTH ** -0.5)
    a_log = jnp.log(jax.random.uniform(ks[7], (DEPTH, DN_HEADS), f, minval=1.0, maxval=16.0))
    dt = jnp.exp(jax.random.uniform(ks[8], (DEPTH, DN_HEADS), f, minval=math.log(1e-3), maxval=math.log(1e-1)))
    dt_bias = dt + jnp.log(-jnp.expm1(-dt))
    dn_head_norm = 1.0 + nrm(ks[9], (DEPTH, DN_HEAD_DIM), 0.02)
    w_dn_proj = nrm(ks[10], (DEPTH, DN_VAL, D_MODEL), DN_VAL ** -0.5)
    w_out = nrm(ks[11], (DEPTH, D_MODEL, D_MODEL), D_MODEL ** -0.5)
    final_norm_gain = 1.0 + nrm(ks[12], (D_MODEL,), 0.02)
    return {"x": x, "norm_gain": norm_gain, "w_in": w_in, "w_qkv_conv": w_qkv_conv,
            "pool_mix": pool_mix, "pool_scale": pool_scale, "w_pool_proj": w_pool_proj,
            "a_log": a_log, "dt_bias": dt_bias, "dn_head_norm": dn_head_norm,
            "w_dn_proj": w_dn_proj, "w_out": w_out, "final_norm_gain": final_norm_gain}


def reference(x, norm_gain, w_in, w_qkv_conv, pool_mix, pool_scale, w_pool_proj,
              a_log, dt_bias, dn_head_norm, w_dn_proj, w_out, final_norm_gain):
    b, s, _ = x.shape
    split_idx = [int(i) for i in np.cumsum(IN_SPLITS)[:-1]]
    h = x
    for layer in range(DEPTH):
        xn = rmsnorm(h, norm_gain[layer])
        proj = jnp.einsum("bsd,dc->bsc", xn, w_in[layer])
        (pool_u, pool_z, q, k, v, dn_z, dec_a, beta_b, gate_pool, gate_dn) = jnp.split(proj, split_idx, axis=-1)

        groups = jnp.split(pool_u, len(POOL_WINDOWS), axis=-1)
        pooled = jnp.stack([causal_mean_minus_self(u_g, w) for u_g, w in zip(groups, POOL_WINDOWS)], axis=2)
        mixed = jnp.einsum("bsgc,gcd->bsgd", pooled, pool_mix[layer]).reshape(b, s, POOL_WIDTH)
        y_pool = mixed * pool_scale[layer] * jax.nn.silu(pool_z)
        y_pool = jnp.einsum("bsc,cd->bsd", y_pool, w_pool_proj[layer])

        qkv = jax.nn.silu(causal_depthwise_conv(jnp.concatenate([q, k, v], axis=-1), w_qkv_conv[layer]))
        qkv = qkv.astype(jnp.float32)
        qh = l2norm(qkv[..., :DN_KEY].reshape(b, s, DN_HEADS, DN_HEAD_DIM)) * (DN_HEAD_DIM ** -0.5)
        kh = l2norm(qkv[..., DN_KEY:2 * DN_KEY].reshape(b, s, DN_HEADS, DN_HEAD_DIM))
        vh = qkv[..., 2 * DN_KEY:].reshape(b, s, DN_HEADS, DN_HEAD_DIM)
        g = -jnp.exp(a_log[layer].astype(jnp.float32)) * jax.nn.softplus(dec_a.astype(jnp.float32) + dt_bias[layer].astype(jnp.float32))
        beta = jax.nn.sigmoid(beta_b.astype(jnp.float32))
        o = gated_delta_rule_chunked(qh, kh, vh, g, beta)
        o = rmsnorm(o, dn_head_norm[layer]).reshape(b, s, DN_VAL).astype(x.dtype)
        y_dn = o * jax.nn.silu(dn_z)
        y_dn = jnp.einsum("bsc,cd->bsd", y_dn, w_dn_proj[layer])

        merged = jax.nn.sigmoid(gate_pool) * y_pool + jax.nn.sigmoid(gate_dn) * y_dn
        h = h + jnp.einsum("bsd,de->bse", merged, w_out[layer])
    return rmsnorm(h, final_norm_gain)
```

```python
import functools

import jax
import jax.numpy as jnp
from jax import lax
from jax.experimental import pallas as pl
from jax.experimental.pallas import tpu as pltpu

NORM_EPS = 1e-6
L2_EPS = 1e-6
POOL_WINDOWS = (2, 4, 8, 16)
CONV_TAPS = 4
CHUNK = 64
INV_BLOCK = 16
LANES = 128
HALO_ROWS = 16
VMEM_LIMIT = 56 * 1024 * 1024

BF16 = jnp.bfloat16
F32 = jnp.float32


def _cparams(*sem):
    return pltpu.CompilerParams(dimension_semantics=sem, vmem_limit_bytes=VMEM_LIMIT)


def _dot(a, b):
    return jnp.dot(a.astype(BF16), b.astype(BF16), preferred_element_type=F32)


def _dot_nt(a, b):
    return lax.dot_general(a.astype(BF16), b.astype(BF16), (((1,), (1,)), ((), ())),
                           preferred_element_type=F32)


def _dot_tn(a, b):
    return lax.dot_general(a.astype(BF16), b.astype(BF16), (((0,), (0,)), ((), ())),
                           preferred_element_type=F32)


def _dot_f32(a, b):
    return jnp.dot(a, b, preferred_element_type=F32, precision=lax.Precision.HIGHEST)


def _silu(x):
    return x * jax.nn.sigmoid(x)


def _rmsnorm_kernel(x_ref, g_ref, o_ref):
    x = x_ref[...]
    ms = jnp.mean(x * x, axis=-1, keepdims=True)
    o_ref[...] = (x * lax.rsqrt(ms + NORM_EPS) * g_ref[...]).astype(o_ref.dtype)


def _rmsnorm(x, gain, rows):
    s, d = x.shape
    return pl.pallas_call(
        _rmsnorm_kernel,
        grid=(s // rows,),
        in_specs=[pl.BlockSpec((rows, d), lambda i: (i, 0)),
                  pl.BlockSpec((1, d), lambda i: (0, 0))],
        out_specs=pl.BlockSpec((rows, d), lambda i: (i, 0)),
        out_shape=jax.ShapeDtypeStruct((s, d), BF16),
        compiler_params=_cparams("parallel"),
        name="rmsnorm_in",
    )(x, gain.reshape(1, d))


def _matmul_kernel(x_ref, w_ref, o_ref):
    o_ref[...] = jnp.dot(x_ref[...], w_ref[...], preferred_element_type=F32).astype(o_ref.dtype)


def _matmul(x, w, tm, tn, out_dtype, name):
    m, k = x.shape
    n = w.shape[1]
    return pl.pallas_call(
        _matmul_kernel,
        grid=(m // tm, n // tn),
        in_specs=[pl.BlockSpec((tm, k), lambda i, j: (i, 0)),
                  pl.BlockSpec((k, tn), lambda i, j: (0, j))],
        out_specs=pl.BlockSpec((tm, tn), lambda i, j: (i, j)),
        out_shape=jax.ShapeDtypeStruct((m, n), out_dtype),
        compiler_params=_cparams("parallel", "parallel"),
        name=name,
    )(x, w)


def _split3(x):
    hi = x.astype(BF16)
    r1 = x - hi.astype(F32)
    mid = r1.astype(BF16)
    lo = (r1 - mid.astype(F32)).astype(BF16)
    return hi, mid, lo


def _gates_kernel(xn_ref, w_ref, alog_ref, dtb_ref, gc_ref, beta_ref, egc_ref, ekd_ref, *, rows, chunk):
    small = jnp.dot(xn_ref[...], w_ref[...], preferred_element_type=F32)
    g = -jnp.exp(alog_ref[...]) * jax.nn.softplus(small + dtb_ref[...])
    beta_ref[...] = jax.nn.sigmoid(small)
    r = lax.broadcasted_iota(jnp.int32, (chunk, chunk), 0)
    c = lax.broadcasted_iota(jnp.int32, (chunk, chunk), 1)
    tri = jnp.where(c <= r, 1.0, 0.0).astype(BF16)
    for n in range(rows // chunk):
        sl = slice(n * chunk, (n + 1) * chunk)
        hi, mid, lo = _split3(g[sl])
        gc = (jnp.dot(tri, hi, preferred_element_type=F32)
              + jnp.dot(tri, mid, preferred_element_type=F32)
              + jnp.dot(tri, lo, preferred_element_type=F32))
        gc_ref[sl, :] = gc
        egc_ref[sl, :] = jnp.exp(gc)
        ekd_ref[sl, :] = jnp.exp(gc[chunk - 1:chunk, :] - gc)


def _gates(xn, w_small, alog, dtb, rows, chunk):
    s, d = xn.shape
    spec = pl.BlockSpec((rows, LANES), lambda i: (i, 0))
    shp = jax.ShapeDtypeStruct((s, LANES), F32)
    return pl.pallas_call(
        functools.partial(_gates_kernel, rows=rows, chunk=chunk),
        grid=(s // rows,),
        in_specs=[pl.BlockSpec((rows, d), lambda i: (i, 0)),
                  pl.BlockSpec((d, LANES), lambda i: (0, 0)),
                  pl.BlockSpec((1, LANES), lambda i: (0, 0)),
                  pl.BlockSpec((1, LANES), lambda i: (0, 0))],
        out_specs=[spec, spec, spec, spec],
        out_shape=[shp, shp, shp, shp],
        compiler_params=_cparams("parallel"),
        name="decay_beta",
    )(xn, w_small, alog, dtb)


def _pool_kernel(u_ref, halo_ref, z_ref, mix_ref, scale_ref, o_ref, *, rows):
    grp = pl.program_id(0)
    i = pl.program_id(1)
    window = jnp.left_shift(2, grp)
    u = u_ref[...]
    r = lax.broadcasted_iota(jnp.int32, (rows, rows), 0)
    c = lax.broadcasted_iota(jnp.int32, (rows, rows), 1)
    band = jnp.where((c <= r) & (c > r - window), 1.0, 0.0).astype(BF16)
    rh = lax.broadcasted_iota(jnp.int32, (rows, HALO_ROWS), 0)
    ch = lax.broadcasted_iota(jnp.int32, (rows, HALO_ROWS), 1) - HALO_ROWS
    bandh = jnp.where((ch > rh - window) & (i > 0), 1.0, 0.0).astype(BF16)
    wsum = (jnp.dot(band, u, preferred_element_type=F32)
            + jnp.dot(bandh, halo_ref[...], preferred_element_type=F32))
    t = i * rows + lax.broadcasted_iota(jnp.int32, (rows, 1), 0)
    count = jnp.minimum(t + 1, window).astype(F32)
    pooled = wsum / count - u.astype(F32)
    mixed = jnp.dot(pooled.astype(BF16), mix_ref[0], preferred_element_type=F32)
    y = mixed * scale_ref[...] * _silu(z_ref[...].astype(F32))
    o_ref[...] = y.astype(o_ref.dtype)


def _pool(proj, mix, scale, rows, width):
    s = proj.shape[0]
    ngrp = len(POOL_WINDOWS)
    gw = width // ngrp
    hb = rows // HALO_ROWS
    return pl.pallas_call(
        functools.partial(_pool_kernel, rows=rows),
        grid=(ngrp, s // rows),
        in_specs=[pl.BlockSpec((rows, gw), lambda g, i: (i, g)),
                  pl.BlockSpec((HALO_ROWS, gw), lambda g, i: (jnp.maximum(i * hb - 1, 0), g)),
                  pl.BlockSpec((rows, gw), lambda g, i: (i, ngrp + g)),
                  pl.BlockSpec((1, gw, gw), lambda g, i: (g, 0, 0)),
                  pl.BlockSpec((1, gw), lambda g, i: (0, g))],
        out_specs=pl.BlockSpec((rows, gw), lambda g, i: (i, g)),
        out_shape=jax.ShapeDtypeStruct((s, width), BF16),
        compiler_params=_cparams("parallel", "parallel"),
        name="pool_mixer",
    )(proj, proj, proj, mix, scale.reshape(1, width))


def _unit_lower_inverse(a, n):
    r = lax.broadcasted_iota(jnp.int32, (n, n), 0)
    c = lax.broadcasted_iota(jnp.int32, (n, n), 1)
    eye = jnp.where(r == c, 1.0, 0.0).astype(F32)
    shift = INV_BLOCK.bit_length() - 1
    b = jnp.where((r >> shift) == (c >> shift), -a, 0.0)
    bp = _dot_f32(b, b)
    t = eye + b + bp + _dot_f32(b, bp)
    power = 4
    while power < INV_BLOCK:
        bp = _dot_f32(bp, bp)
        t = t + _dot_f32(t, bp)
        power *= 2
    while (1 << shift) < n:
        off = ((r >> (shift + 1)) == (c >> (shift + 1))) & ((r >> shift) != (c >> shift))
        x = _dot_f32(jnp.where(off, a, 0.0), t)
        t = t - _dot_f32(t, x)
        shift += 1
    return t


def _causal_conv(raw_ref, halo_ref, w_ref, first):
    rows = raw_ref.shape[0]
    halo = jnp.where(first, 0.0, halo_ref[HALO_ROWS - 8:, :].astype(F32))
    xcat = jnp.concatenate([halo, raw_ref[...].astype(F32)], axis=0)
    w = w_ref[...]
    out = xcat[8:] * w[CONV_TAPS - 1:CONV_TAPS]
    for back in range(1, CONV_TAPS):
        shifted = pltpu.roll(xcat, back, axis=0)[8:]
        out = out + shifted * w[CONV_TAPS - 1 - back:CONV_TAPS - back]
    return out


def _deltanet_kernel(q_ref, k_ref, v_ref, qh_ref, kh_ref, vh_ref, wq_ref, wk_ref, wv_ref,
                     col_ref, row_ref, z_ref, hn_ref, o_ref, state_ref, *, heads, rows, chunk, hdim):
    i = pl.program_id(1)
    first = i == 0

    @pl.when(first)
    def _():
        state_ref[...] = jnp.zeros_like(state_ref)

    q_all = _silu(_causal_conv(q_ref, qh_ref, wq_ref, first))
    k_all = _silu(_causal_conv(k_ref, kh_ref, wk_ref, first))
    v_all = _silu(_causal_conv(v_ref, vh_ref, wv_ref, first))
    col = col_ref[0]
    rowp = row_ref[0]
    r = lax.broadcasted_iota(jnp.int32, (chunk, chunk), 0)
    c = lax.broadcasted_iota(jnp.int32, (chunk, chunk), 1)
    causal = c <= r
    strict = c < r
    hn = hn_ref[...]

    for h in range(heads):
        hs = slice(h * hdim, (h + 1) * hdim)
        qh = q_all[:, hs]
        kh = k_all[:, hs]
        vh = v_all[:, hs]
        qh = qh * lax.rsqrt(jnp.sum(qh * qh, axis=-1, keepdims=True) + L2_EPS) * (hdim ** -0.5)
        kh = kh * lax.rsqrt(jnp.sum(kh * kh, axis=-1, keepdims=True) + L2_EPS)
        beta_c = col[:, 4 * h + 0:4 * h + 1]
        gc_c = col[:, 4 * h + 1:4 * h + 2]
        egc_c = col[:, 4 * h + 2:4 * h + 3]
        ekd_c = col[:, 4 * h + 3:4 * h + 4]
        state = state_ref[h]
        for n in range(rows // chunk):
            cs = slice(n * chunk, (n + 1) * chunk)
            q, k, v = qh[cs], kh[cs], vh[cs]
            beta, egc, ekd = beta_c[cs], egc_c[cs], ekd_c[cs]
            diff = gc_c[cs] - rowp[h:h + 1, cs]
            decay = jnp.where(causal, jnp.exp(jnp.where(causal, diff, 0.0)), 0.0)
            kb = k * beta
            a = jnp.where(strict, _dot_nt(kb, k) * decay, 0.0)
            t = _unit_lower_inverse(a, chunk)
            sol = _dot_f32(t, jnp.concatenate([v * beta, kb * egc], axis=1))
            u, w = sol[:, :hdim], sol[:, hdim:]
            qk = jnp.where(causal, _dot_nt(q, k) * decay, 0.0)
            ws = _dot(jnp.concatenate([w, q * egc], axis=0), state)
            v_new = u - ws[:chunk]
            o = ws[chunk:] + _dot(qk, v_new)
            state = state * egc[chunk - 1:chunk, :] + _dot_tn(k * ekd, v_new)
            o = o * lax.rsqrt(jnp.mean(o * o, axis=-1, keepdims=True) + NORM_EPS) * hn
            y = o * _silu(z_ref[cs, hs].astype(F32))
            o_ref[cs, hs] = y.astype(o_ref.dtype)
        state_ref[h] = state


def _deltanet(proj, conv_w, colpack, rowpack, head_norm, n_heads, hdim, heads, rows, chunk, q_col):
    s = proj.shape[0]
    bw = heads * hdim
    hb = rows // HALO_ROWS
    nq = q_col // bw
    nkv = (n_heads * hdim) // bw

    def main(off):
        return pl.BlockSpec((rows, bw), lambda g, i: (i, off + g))

    def halo(off):
        return pl.BlockSpec((HALO_ROWS, bw), lambda g, i: (jnp.maximum(i * hb - 1, 0), off + g))

    def taps(off):
        return pl.BlockSpec((CONV_TAPS, bw), lambda g, i: (0, off + g))

    return pl.pallas_call(
        functools.partial(_deltanet_kernel, heads=heads, rows=rows, chunk=chunk, hdim=hdim),
        grid=(n_heads // heads, s // rows),
        in_specs=[main(nq), main(nq + nkv), main(nq + 2 * nkv),
                  halo(nq), halo(nq + nkv), halo(nq + 2 * nkv),
                  taps(0), taps(nkv), taps(2 * nkv),
                  pl.BlockSpec((1, rows, LANES), lambda g, i: (g, i, 0)),
                  pl.BlockSpec((1, 8, rows), lambda g, i: (g, 0, i)),
                  main(nq + 3 * nkv),
                  pl.BlockSpec((1, hdim), lambda g, i: (0, 0))],
        out_specs=pl.BlockSpec((rows, bw), lambda g, i: (i, g)),
        out_shape=jax.ShapeDtypeStruct((s, n_heads * hdim), BF16),
        scratch_shapes=[pltpu.VMEM((heads, hdim, hdim), F32)],
        compiler_params=_cparams("parallel", "arbitrary"),
        name="deltanet",
    )(proj, proj, proj, proj, proj, proj, conv_w, conv_w, conv_w,
      colpack, rowpack, proj, head_norm.reshape(1, hdim))


def _merge_kernel(yp_ref, yd_ref, wp_ref, wd_ref, gp_ref, gd_ref, o_ref):
    yp = jnp.dot(yp_ref[...], wp_ref[...], preferred_element_type=F32)
    yd = jnp.dot(yd_ref[...], wd_ref[...], preferred_element_type=F32)
    merged = (jax.nn.sigmoid(gp_ref[...].astype(F32)) * yp
              + jax.nn.sigmoid(gd_ref[...].astype(F32)) * yd)
    o_ref[...] = merged.astype(o_ref.dtype)


def _merge(y_pool, y_dn, w_pool, w_dn, proj, gate_col, tm, tn):
    s, d = y_pool.shape
    n = w_pool.shape[1]
    gp0 = gate_col // tn
    gd0 = (gate_col + n) // tn
    return pl.pallas_call(
        _merge_kernel,
        grid=(s // tm, n // tn),
        in_specs=[pl.BlockSpec((tm, d), lambda i, j: (i, 0)),
                  pl.BlockSpec((tm, d), lambda i, j: (i, 0)),
                  pl.BlockSpec((d, tn), lambda i, j: (0, j)),
                  pl.BlockSpec((d, tn), lambda i, j: (0, j)),
                  pl.BlockSpec((tm, tn), lambda i, j: (i, gp0 + j)),
                  pl.BlockSpec((tm, tn), lambda i, j: (i, gd0 + j))],
        out_specs=pl.BlockSpec((tm, tn), lambda i, j: (i, j)),
        out_shape=jax.ShapeDtypeStruct((s, n), BF16),
        compiler_params=_cparams("parallel", "parallel"),
        name="branch_merge",
    )(y_pool, y_dn, w_pool, w_dn, proj, proj)


def _out_kernel(m_ref, w_ref, x_ref, g_ref, o_ref, *, tn):
    j = pl.program_id(1)
    h = x_ref[...] + jnp.dot(m_ref[...], w_ref[...], preferred_element_type=F32)
    o_ref[:, pl.ds(pl.multiple_of(j * tn, LANES), tn)] = h

    @pl.when(j == pl.num_programs(1) - 1)
    def _():
        hh = o_ref[...]
        ms = jnp.mean(hh * hh, axis=-1, keepdims=True)
        o_ref[...] = hh * lax.rsqrt(ms + NORM_EPS) * g_ref[...]


def _out_proj(merged, w_out, x, gain, tm, tn):
    s, d = x.shape
    return pl.pallas_call(
        functools.partial(_out_kernel, tn=tn),
        grid=(s // tm, d // tn),
        in_specs=[pl.BlockSpec((tm, d), lambda i, j: (i, 0)),
                  pl.BlockSpec((d, tn), lambda i, j: (0, j)),
                  pl.BlockSpec((tm, tn), lambda i, j: (i, j)),
                  pl.BlockSpec((1, d), lambda i, j: (0, 0))],
        out_specs=pl.BlockSpec((tm, d), lambda i, j: (i, 0)),
        out_shape=jax.ShapeDtypeStruct((s, d), F32),
        compiler_params=_cparams("parallel", "arbitrary"),
        name="out_proj_norm",
    )(merged, w_out, x, gain.reshape(1, d))


def _layer(h, norm_gain, w_in, conv_w, pool_mix, pool_scale, w_pool_proj, a_log, dt_bias,
           head_norm, w_dn_proj, w_out, out_gain):
    s, d = h.shape
    n_heads = a_log.shape[0]
    hdim = head_norm.shape[0]
    width = pool_scale.shape[0]
    dn = n_heads * hdim
    small0 = 2 * width + 4 * dn
    small1 = small0 + 2 * n_heads
    w_big = jnp.concatenate([w_in[:, :small0], w_in[:, small1:]], axis=1).astype(BF16)
    w_small = jnp.pad(w_in[:, small0:small1], ((0, 0), (0, LANES - 2 * n_heads))).astype(BF16)

    xn = _rmsnorm(h, norm_gain, min(256, s))
    tm = min(1024, s)
    proj = _matmul(xn, w_big, tm, min(1024, d), BF16, "in_proj")

    pad = LANES - n_heads
    alog = jnp.pad(a_log, (0, pad)).reshape(1, LANES)
    dtb = jnp.pad(dt_bias, (0, pad)).reshape(1, LANES)
    gc, beta, egc, ekd = _gates(xn, w_small, alog, dtb, min(256, s), CHUNK)
    heads = min(4, n_heads)
    ngrp = n_heads // heads
    cols = jnp.stack([beta[:, n_heads:2 * n_heads], gc[:, :n_heads],
                      egc[:, :n_heads], ekd[:, :n_heads]], axis=-1)
    colpack = cols.reshape(s, ngrp, heads * 4).transpose(1, 0, 2)
    colpack = jnp.pad(colpack, ((0, 0), (0, 0), (0, LANES - heads * 4)))
    rowpack = gc[:, :n_heads].T.reshape(ngrp, heads, s)
    rowpack = jnp.pad(rowpack, ((0, 0), (0, 8 - heads), (0, 0)))

    y_pool = _pool(proj, pool_mix.astype(BF16), pool_scale, min(256, s), width)
    y_dn = _deltanet(proj, conv_w, colpack, rowpack, head_norm, n_heads, hdim,
                     heads, min(256, s), CHUNK, 2 * width)
    merged = _merge(y_pool, y_dn, w_pool_proj.astype(BF16), w_dn_proj.astype(BF16), proj,
                    small0, min(512, s), min(512, d))
    return _out_proj(merged, w_out.astype(BF16), h, out_gain, min(512, s), min(512, d))


def kernel(x, norm_gain, w_in, w_qkv_conv, pool_mix, pool_scale, w_pool_proj, a_log, dt_bias,
           dn_head_norm, w_dn_proj, w_out, final_norm_gain):
    b, s, d = x.shape
    depth = norm_gain.shape[0]
    assert depth == 1, "the final RMSNorm is fused into the (single) layer's output projection"
    outs = []
    for bi in range(b):
        outs.append(_layer(x[bi], norm_gain[0], w_in[0], w_qkv_conv[0], pool_mix[0], pool_scale[0],
                           w_pool_proj[0], a_log[0], dt_bias[0], dn_head_norm[0], w_dn_proj[0],
                           w_out[0], final_norm_gain))
    return jnp.stack(outs, axis=0)
```

```python
import functools

import jax
import jax.numpy as jnp
from jax import lax
from jax.experimental import pallas as pl
from jax.experimental.pallas import tpu as pltpu

NORM_EPS = 1e-6
L2_EPS = 1e-6
POOL_WINDOWS = (2, 4, 8, 16)
CONV_TAPS = 4
CHUNK = 64
INV_BLOCK = 16
LANES = 128
HALO_ROWS = 16
VMEM_LIMIT = 56 * 1024 * 1024

BF16 = jnp.bfloat16
F32 = jnp.float32


def _cparams(*sem):
    return pltpu.CompilerParams(dimension_semantics=sem, vmem_limit_bytes=VMEM_LIMIT)


def _dot(a, b):
    return jnp.dot(a.astype(BF16), b.astype(BF16), preferred_element_type=F32)


def _dot_nt(a, b):
    return lax.dot_general(a.astype(BF16), b.astype(BF16), (((1,), (1,)), ((), ())),
                           preferred_element_type=F32)


def _dot_tn(a, b):
    return lax.dot_general(a.astype(BF16), b.astype(BF16), (((0,), (0,)), ((), ())),
                           preferred_element_type=F32)


def _silu(x):
    return x * jax.nn.sigmoid(x)


def _rmsnorm_kernel(x_ref, g_ref, o_ref):
    x = x_ref[...]
    ms = jnp.mean(x * x, axis=-1, keepdims=True)
    o_ref[...] = (x * lax.rsqrt(ms + NORM_EPS) * g_ref[...]).astype(o_ref.dtype)


def _rmsnorm(x, gain, rows):
    s, d = x.shape
    return pl.pallas_call(
        _rmsnorm_kernel,
        grid=(s // rows,),
        in_specs=[pl.BlockSpec((rows, d), lambda i: (i, 0)),
                  pl.BlockSpec((1, d), lambda i: (0, 0))],
        out_specs=pl.BlockSpec((rows, d), lambda i: (i, 0)),
        out_shape=jax.ShapeDtypeStruct((s, d), BF16),
        compiler_params=_cparams("parallel"),
        name="rmsnorm_in",
    )(x, gain.reshape(1, d))


def _matmul_kernel(x_ref, w_ref, o_ref):
    o_ref[...] = jnp.dot(x_ref[...], w_ref[...], preferred_element_type=F32).astype(o_ref.dtype)


def _matmul(x, w, tm, tn, out_dtype, name):
    m, k = x.shape
    n = w.shape[1]
    return pl.pallas_call(
        _matmul_kernel,
        grid=(m // tm, n // tn),
        in_specs=[pl.BlockSpec((tm, k), lambda i, j: (i, 0)),
                  pl.BlockSpec((k, tn), lambda i, j: (0, j))],
        out_specs=pl.BlockSpec((tm, tn), lambda i, j: (i, j)),
        out_shape=jax.ShapeDtypeStruct((m, n), out_dtype),
        compiler_params=_cparams("parallel", "parallel"),
        name=name,
    )(x, w)


def _split3(x):
    hi = x.astype(BF16)
    r1 = x - hi.astype(F32)
    mid = r1.astype(BF16)
    lo = (r1 - mid.astype(F32)).astype(BF16)
    return hi, mid, lo


def _gates_kernel(xn_ref, w_ref, alog_ref, dtb_ref, gc_ref, beta_ref, egc_ref, ekd_ref, *, rows, chunk):
    small = jnp.dot(xn_ref[...], w_ref[...], preferred_element_type=F32)
    g = -jnp.exp(alog_ref[...]) * jax.nn.softplus(small + dtb_ref[...])
    beta_ref[...] = jax.nn.sigmoid(small)
    r = lax.broadcasted_iota(jnp.int32, (chunk, chunk), 0)
    c = lax.broadcasted_iota(jnp.int32, (chunk, chunk), 1)
    tri = jnp.where(c <= r, 1.0, 0.0).astype(BF16)
    for n in range(rows // chunk):
        sl = slice(n * chunk, (n + 1) * chunk)
        hi, mid, lo = _split3(g[sl])
        gc = (jnp.dot(tri, hi, preferred_element_type=F32)
              + jnp.dot(tri, mid, preferred_element_type=F32)
              + jnp.dot(tri, lo, preferred_element_type=F32))
        gc_ref[sl, :] = gc
        egc_ref[sl, :] = jnp.exp(gc)
        ekd_ref[sl, :] = jnp.exp(gc[chunk - 1:chunk, :] - gc)


def _gates(xn, w_small, alog, dtb, rows, chunk):
    s, d = xn.shape
    spec = pl.BlockSpec((rows, LANES), lambda i: (i, 0))
    shp = jax.ShapeDtypeStruct((s, LANES), F32)
    return pl.pallas_call(
        functools.partial(_gates_kernel, rows=rows, chunk=chunk),
        grid=(s // rows,),
        in_specs=[pl.BlockSpec((rows, d), lambda i: (i, 0)),
                  pl.BlockSpec((d, LANES), lambda i: (0, 0)),
                  pl.BlockSpec((1, LANES), lambda i: (0, 0)),
                  pl.BlockSpec((1, LANES), lambda i: (0, 0))],
        out_specs=[spec, spec, spec, spec],
        out_shape=[shp, shp, shp, shp],
        compiler_params=_cparams("parallel"),
        name="decay_beta",
    )(xn, w_small, alog, dtb)


def _pool_kernel(u_ref, halo_ref, z_ref, mix_ref, scale_ref, o_ref, *, rows):
    grp = pl.program_id(0)
    i = pl.program_id(1)
    window = jnp.left_shift(2, grp)
    u = u_ref[...]
    r = lax.broadcasted_iota(jnp.int32, (rows, rows), 0)
    c = lax.broadcasted_iota(jnp.int32, (rows, rows), 1)
    band = jnp.where((c <= r) & (c > r - window), 1.0, 0.0).astype(BF16)
    rh = lax.broadcasted_iota(jnp.int32, (rows, HALO_ROWS), 0)
    ch = lax.broadcasted_iota(jnp.int32, (rows, HALO_ROWS), 1) - HALO_ROWS
    bandh = jnp.where((ch > rh - window) & (i > 0), 1.0, 0.0).astype(BF16)
    wsum = (jnp.dot(band, u, preferred_element_type=F32)
            + jnp.dot(bandh, halo_ref[...], preferred_element_type=F32))
    t = i * rows + lax.broadcasted_iota(jnp.int32, (rows, 1), 0)
    count = jnp.minimum(t + 1, window).astype(F32)
    pooled = wsum / count - u.astype(F32)
    mixed = jnp.dot(pooled.astype(BF16), mix_ref[0], preferred_element_type=F32)
    y = mixed * scale_ref[...] * _silu(z_ref[...].astype(F32))
    o_ref[...] = y.astype(o_ref.dtype)


def _pool(proj, mix, scale, rows, width):
    s = proj.shape[0]
    ngrp = len(POOL_WINDOWS)
    gw = width // ngrp
    hb = rows // HALO_ROWS
    return pl.pallas_call(
        functools.partial(_pool_kernel, rows=rows),
        grid=(ngrp, s // rows),
        in_specs=[pl.BlockSpec((rows, gw), lambda g, i: (i, g)),
                  pl.BlockSpec((HALO_ROWS, gw), lambda g, i: (jnp.maximum(i * hb - 1, 0), g)),
                  pl.BlockSpec((rows, gw), lambda g, i: (i, ngrp + g)),
                  pl.BlockSpec((1, gw, gw), lambda g, i: (g, 0, 0)),
                  pl.BlockSpec((1, gw), lambda g, i: (0, g))],
        out_specs=pl.BlockSpec((rows, gw), lambda g, i: (i, g)),
        out_shape=jax.ShapeDtypeStruct((s, width), BF16),
        compiler_params=_cparams("parallel", "parallel"),
        name="pool_mixer",
    )(proj, proj, proj, mix, scale.reshape(1, width))


def _unit_lower_inverse(mats, n):
    r = lax.broadcasted_iota(jnp.int32, (n, n), 0)
    c = lax.broadcasted_iota(jnp.int32, (n, n), 1)
    eye = jnp.where(r == c, 1.0, 0.0).astype(F32)
    shift = INV_BLOCK.bit_length() - 1
    diag = (r >> shift) == (c >> shift)
    bs = [jnp.where(diag, -a, 0.0) for a in mats]
    bps = [_dot(b, b) for b in bs]
    ts = [eye + b + bp + _dot(b, bp) for b, bp in zip(bs, bps)]
    power = 4
    while power < INV_BLOCK:
        bps = [_dot(bp, bp) for bp in bps]
        ts = [t + _dot(t, bp) for t, bp in zip(ts, bps)]
        power *= 2
    while (1 << shift) < n:
        off = ((r >> (shift + 1)) == (c >> (shift + 1))) & ((r >> shift) != (c >> shift))
        xs = [_dot(jnp.where(off, a, 0.0), t) for a, t in zip(mats, ts)]
        ts = [t - _dot(t, x) for t, x in zip(ts, xs)]
        shift += 1
    return ts


def _causal_conv(raw_ref, halo_ref, w_ref, first):
    rows = raw_ref.shape[0]
    halo = jnp.where(first, 0.0, halo_ref[HALO_ROWS - 8:, :].astype(F32))
    xcat = jnp.concatenate([halo, raw_ref[...].astype(F32)], axis=0)
    w = w_ref[...]
    out = xcat[8:] * w[CONV_TAPS - 1:CONV_TAPS]
    for back in range(1, CONV_TAPS):
        shifted = pltpu.roll(xcat, back, axis=0)[8:]
        out = out + shifted * w[CONV_TAPS - 1 - back:CONV_TAPS - back]
    return out


def _deltanet_kernel(q_ref, k_ref, v_ref, qh_ref, kh_ref, vh_ref, wq_ref, wk_ref, wv_ref,
                     col_ref, row_ref, z_ref, hn_ref, o_ref, state_ref, *, heads, rows, chunk, hdim):
    i = pl.program_id(1)
    first = i == 0

    @pl.when(first)
    def _():
        state_ref[...] = jnp.zeros_like(state_ref)

    q_all = _silu(_causal_conv(q_ref, qh_ref, wq_ref, first))
    k_all = _silu(_causal_conv(k_ref, kh_ref, wk_ref, first))
    v_all = _silu(_causal_conv(v_ref, vh_ref, wv_ref, first))
    col = col_ref[0]
    rowp = row_ref[0]
    r = lax.broadcasted_iota(jnp.int32, (chunk, chunk), 0)
    c = lax.broadcasted_iota(jnp.int32, (chunk, chunk), 1)
    causal = c <= r
    strict = c < r
    hn = hn_ref[...]
    nchunk = rows // chunk

    probs = [(n, h) for n in range(nchunk) for h in range(heads)]
    qn, kn = {}, {}
    for h in range(heads):
        hs = slice(h * hdim, (h + 1) * hdim)
        qh, kh = q_all[:, hs], k_all[:, hs]
        qn[h] = qh * (lax.rsqrt(jnp.sum(qh * qh, axis=-1, keepdims=True) + L2_EPS) * (hdim ** -0.5))
        kn[h] = kh * lax.rsqrt(jnp.sum(kh * kh, axis=-1, keepdims=True) + L2_EPS)
    k_, kb_, rhs_, decay_, qg_, kd_, egl_ = {}, {}, {}, {}, {}, {}, {}
    for n, h in probs:
        cs = slice(n * chunk, (n + 1) * chunk)
        beta = col[cs, 4 * h + 0:4 * h + 1]
        gc = col[cs, 4 * h + 1:4 * h + 2]
        egc = col[cs, 4 * h + 2:4 * h + 3]
        ekd = col[cs, 4 * h + 3:4 * h + 4]
        k = kn[h][cs]
        kb = k * beta
        diff = gc - rowp[h:h + 1, cs]
        decay_[n, h] = jnp.where(causal, jnp.exp(jnp.where(causal, diff, 0.0)), 0.0)
        k_[n, h], kb_[n, h] = k, kb
        rhs_[n, h] = jnp.concatenate([v_all[cs, h * hdim:(h + 1) * hdim] * beta, kb * egc], axis=1)
        qg_[n, h] = qn[h][cs] * egc
        kd_[n, h] = k * ekd
        egl_[n, h] = egc[chunk - 1:chunk, :]
    kq_ = {p: _dot_nt(jnp.concatenate([kb_[p], qn[p[1]][p[0] * chunk:(p[0] + 1) * chunk]], axis=0), k_[p])
           for p in probs}
    a_ = [jnp.where(strict, kq_[p][:chunk] * decay_[p], 0.0) for p in probs]
    qk_ = {p: jnp.where(causal, kq_[p][chunk:] * decay_[p], 0.0) for p in probs}
    t_ = dict(zip(probs, _unit_lower_inverse(a_, chunk)))
    sol_ = {p: _dot(t_[p], rhs_[p]) for p in probs}

    states = [state_ref[h] for h in range(heads)]
    for n in range(nchunk):
        cs = slice(n * chunk, (n + 1) * chunk)
        ws = [_dot(jnp.concatenate([sol_[n, h][:, hdim:], qg_[n, h]], axis=0), states[h])
              for h in range(heads)]
        v_new = [sol_[n, h][:, :hdim] - ws[h][:chunk] for h in range(heads)]
        o = [ws[h][chunk:] + _dot(qk_[n, h], v_new[h]) for h in range(heads)]
        states = [states[h] * egl_[n, h] + _dot_tn(kd_[n, h], v_new[h]) for h in range(heads)]
        for h in range(heads):
            hs = slice(h * hdim, (h + 1) * hdim)
            oh = o[h] * lax.rsqrt(jnp.mean(o[h] * o[h], axis=-1, keepdims=True) + NORM_EPS) * hn
            o_ref[cs, hs] = (oh * _silu(z_ref[cs, hs].astype(F32))).astype(o_ref.dtype)
    for h in range(heads):
        state_ref[h] = states[h]


def _deltanet(proj, conv_w, colpack, rowpack, head_norm, n_heads, hdim, heads, rows, chunk, q_col):
    s = proj.shape[0]
    bw = heads * hdim
    hb = rows // HALO_ROWS
    nq = q_col // bw
    nkv = (n_heads * hdim) // bw

    def main(off):
        return pl.BlockSpec((rows, bw), lambda g, i: (i, off + g))

    def halo(off):
        return pl.BlockSpec((HALO_ROWS, bw), lambda g, i: (jnp.maximum(i * hb - 1, 0), off + g))

    def taps(off):
        return pl.BlockSpec((CONV_TAPS, bw), lambda g, i: (0, off + g))

    return pl.pallas_call(
        functools.partial(_deltanet_kernel, heads=heads, rows=rows, chunk=chunk, hdim=hdim),
        grid=(n_heads // heads, s // rows),
        in_specs=[main(nq), main(nq + nkv), main(nq + 2 * nkv),
                  halo(nq), halo(nq + nkv), halo(nq + 2 * nkv),
                  taps(0), taps(nkv), taps(2 * nkv),
                  pl.BlockSpec((1, rows, LANES), lambda g, i: (g, i, 0)),
                  pl.BlockSpec((1, 8, rows), lambda g, i: (g, 0, i)),
                  main(nq + 3 * nkv),
                  pl.BlockSpec((1, hdim), lambda g, i: (0, 0))],
        out_specs=pl.BlockSpec((rows, bw), lambda g, i: (i, g)),
        out_shape=jax.ShapeDtypeStruct((s, n_heads * hdim), BF16),
        scratch_shapes=[pltpu.VMEM((heads, hdim, hdim), F32)],
        compiler_params=_cparams("parallel", "arbitrary"),
        name="deltanet",
    )(proj, proj, proj, proj, proj, proj, conv_w, conv_w, conv_w,
      colpack, rowpack, proj, head_norm.reshape(1, hdim))


def _merge_kernel(yp_ref, yd_ref, wp_ref, wd_ref, gp_ref, gd_ref, o_ref):
    yp = jnp.dot(yp_ref[...], wp_ref[...], preferred_element_type=F32)
    yd = jnp.dot(yd_ref[...], wd_ref[...], preferred_element_type=F32)
    merged = (jax.nn.sigmoid(gp_ref[...].astype(F32)) * yp
              + jax.nn.sigmoid(gd_ref[...].astype(F32)) * yd)
    o_ref[...] = merged.astype(o_ref.dtype)


def _merge(y_pool, y_dn, w_pool, w_dn, proj, gate_col, tm, tn):
    s, d = y_pool.shape
    n = w_pool.shape[1]
    gp0 = gate_col // tn
    gd0 = (gate_col + n) // tn
    return pl.pallas_call(
        _merge_kernel,
        grid=(s // tm, n // tn),
        in_specs=[pl.BlockSpec((tm, d), lambda i, j: (i, 0)),
                  pl.BlockSpec((tm, d), lambda i, j: (i, 0)),
                  pl.BlockSpec((d, tn), lambda i, j: (0, j)),
                  pl.BlockSpec((d, tn), lambda i, j: (0, j)),
                  pl.BlockSpec((tm, tn), lambda i, j: (i, gp0 + j)),
                  pl.BlockSpec((tm, tn), lambda i, j: (i, gd0 + j))],
        out_specs=pl.BlockSpec((tm, tn), lambda i, j: (i, j)),
        out_shape=jax.ShapeDtypeStruct((s, n), BF16),
        compiler_params=_cparams("parallel", "parallel"),
        name="branch_merge",
    )(y_pool, y_dn, w_pool, w_dn, proj, proj)


def _out_kernel(m_ref, w_ref, x_ref, g_ref, o_ref, *, tn):
    j = pl.program_id(1)
    h = x_ref[...] + jnp.dot(m_ref[...], w_ref[...], preferred_element_type=F32)
    o_ref[:, pl.ds(pl.multiple_of(j * tn, LANES), tn)] = h

    @pl.when(j == pl.num_programs(1) - 1)
    def _():
        hh = o_ref[...]
        ms = jnp.mean(hh * hh, axis=-1, keepdims=True)
        o_ref[...] = hh * lax.rsqrt(ms + NORM_EPS) * g_ref[...]


def _out_proj(merged, w_out, x, gain, tm, tn):
    s, d = x.shape
    return pl.pallas_call(
        functools.partial(_out_kernel, tn=tn),
        grid=(s // tm, d // tn),
        in_specs=[pl.BlockSpec((tm, d), lambda i, j: (i, 0)),
                  pl.BlockSpec((d, tn), lambda i, j: (0, j)),
                  pl.BlockSpec((tm, tn), lambda i, j: (i, j)),
                  pl.BlockSpec((1, d), lambda i, j: (0, 0))],
        out_specs=pl.BlockSpec((tm, d), lambda i, j: (i, 0)),
        out_shape=jax.ShapeDtypeStruct((s, d), F32),
        compiler_params=_cparams("parallel", "arbitrary"),
        name="out_proj_norm",
    )(merged, w_out, x, gain.reshape(1, d))


def _layer(h, norm_gain, w_in, conv_w, pool_mix, pool_scale, w_pool_proj, a_log, dt_bias,
           head_norm, w_dn_proj, w_out, out_gain):
    s, d = h.shape
    n_heads = a_log.shape[0]
    hdim = head_norm.shape[0]
    width = pool_scale.shape[0]
    dn = n_heads * hdim
    small0 = 2 * width + 4 * dn
    small1 = small0 + 2 * n_heads
    w_big = jnp.concatenate([w_in[:, :small0], w_in[:, small1:]], axis=1).astype(BF16)
    w_small = jnp.pad(w_in[:, small0:small1], ((0, 0), (0, LANES - 2 * n_heads))).astype(BF16)

    xn = _rmsnorm(h, norm_gain, min(256, s))
    tm = min(1024, s)
    proj = _matmul(xn, w_big, tm, min(1024, d), BF16, "in_proj")

    pad = LANES - n_heads
    alog = jnp.pad(a_log, (0, pad)).reshape(1, LANES)
    dtb = jnp.pad(dt_bias, (0, pad)).reshape(1, LANES)
    gc, beta, egc, ekd = _gates(xn, w_small, alog, dtb, min(256, s), CHUNK)
    heads = min(4, n_heads)
    ngrp = n_heads // heads
    cols = jnp.stack([beta[:, n_heads:2 * n_heads], gc[:, :n_heads],
                      egc[:, :n_heads], ekd[:, :n_heads]], axis=-1)
    colpack = cols.reshape(s, ngrp, heads * 4).transpose(1, 0, 2)
    colpack = jnp.pad(colpack, ((0, 0), (0, 0), (0, LANES - heads * 4)))
    rowpack = gc[:, :n_heads].T.reshape(ngrp, heads, s)
    rowpack = jnp.pad(rowpack, ((0, 0), (0, 8 - heads), (0, 0)))

    y_pool = _pool(proj, pool_mix.astype(BF16), pool_scale, min(256, s), width)
    y_dn = _deltanet(proj, conv_w, colpack, rowpack, head_norm, n_heads, hdim,
                     heads, min(256, s), CHUNK, 2 * width)
    merged = _merge(y_pool, y_dn, w_pool_proj.astype(BF16), w_dn_proj.astype(BF16), proj,
                    small0, min(512, s), min(512, d))
    return _out_proj(merged, w_out.astype(BF16), h, out_gain, min(512, s), min(512, d))


def kernel(x, norm_gain, w_in, w_qkv_conv, pool_mix, pool_scale, w_pool_proj, a_log, dt_bias,
           dn_head_norm, w_dn_proj, w_out, final_norm_gain):
    b, s, d = x.shape
    depth = norm_gain.shape[0]
    assert depth == 1, "the final RMSNorm is fused into the (single) layer's output projection"
    outs = []
    for bi in range(b):
        outs.append(_layer(x[bi], norm_gain[0], w_in[0], w_qkv_conv[0], pool_mix[0], pool_scale[0],
                           w_pool_proj[0], a_log[0], dt_bias[0], dn_head_norm[0], w_dn_proj[0],
                           w_out[0], final_norm_gain))
    return jnp.stack(outs, axis=0)
```
